```python
import math
import jax, jax.numpy as jnp
from jax import lax
import numpy as np

D_MODEL = 1024
BATCH = 8
SEQ = 4096
DEPTH = 4

HEAD_DIM = 64
GRID_W = 64
NA_HEADS = 4
NA_KH = 8
NA_KW = 16
NA_QC = 16
DIL_GROUPS = ((128, 1), (512, 4), (2048, 16))
DIL_HEADS_PER_GROUP = 2
DIL_HEADS = 6
DIL_QB = 64
DIFF_HEADS = 4
DIFF_DIM = 48
DIFF_QB = 128
A_W = NA_HEADS * HEAD_DIM
B_W = DIL_HEADS * HEAD_DIM
B_OUT = DIL_HEADS_PER_GROUP * HEAD_DIM
C_W = DIFF_HEADS * 2 * DIFF_DIM
N_BRANCH = 3
IN_COLS = 3 * A_W + 3 * B_W + 3 * C_W + N_BRANCH * D_MODEL
N_EXPERTS = 32
TOP_K = 4
D_EXPERT = 1024
SWIGLU_ALPHA = 1.702
SWIGLU_LIMIT = 7.0
ROPE_THETA = 10000.0
LN_EPS = 1e-5
NEG_INF = -1e30
DEEPNORM_ALPHA = (2 * DEPTH) ** 0.25
DEEPNORM_BETA = (8 * DEPTH) ** -0.25

kernel_name = "hybrid_natten_dilated_diffattn_moe_encoder"

F32 = jnp.float32


def rope_tables(seq, dim):
    inv = 1.0 / (ROPE_THETA ** (jnp.arange(0, dim, 2, dtype=F32) / dim))
    ang = jnp.arange(seq, dtype=F32)[:, None] * inv[None, :]
    return jnp.cos(ang), jnp.sin(ang)


def apply_rope(x, cos, sin):
    x1, x2 = jnp.split(x, 2, axis=-1)
    c = cos.astype(x.dtype)
    s = sin.astype(x.dtype)
    return jnp.concatenate([x1 * c - x2 * s, x2 * c + x1 * s], axis=-1)


def layer_norm(x, g, b):
    xf = x.astype(F32)
    mu = jnp.mean(xf, axis=-1, keepdims=True)
    var = jnp.mean(jnp.square(xf - mu), axis=-1, keepdims=True)
    y = (xf - mu) * lax.rsqrt(var + LN_EPS) * g.astype(F32) + b.astype(F32)
    return y.astype(x.dtype)


def neighborhood_attention(q, k, v, rpb):
    B, H, S, hd = q.shape
    rows = S // GRID_W
    kh = min(NA_KH, rows)
    ncb = GRID_W // NA_QC
    kbw = NA_QC + NA_KW
    r = np.arange(rows)
    row_idx = np.clip(r - kh // 2, 0, rows - kh)[:, None] + np.arange(kh)[None, :]
    cb = np.arange(ncb)
    col_idx = np.clip(cb * NA_QC - NA_KW // 2, 0, GRID_W - kbw)[:, None] + np.arange(kbw)[None, :]
    q_col = cb[:, None] * NA_QC + np.arange(NA_QC)[None, :]
    c_start = np.clip(q_col - NA_KW // 2, 0, GRID_W - NA_KW)
    kc = col_idx[:, None, :]
    col_ok = (kc >= c_start[:, :, None]) & (kc < c_start[:, :, None] + NA_KW)
    row_off = row_idx - r[:, None] + NA_KH - 1
    col_off = np.clip(kc - q_col[:, :, None] + NA_KW - 1, 0, 2 * NA_KW - 2)
    bias = rpb[:, row_off[:, None, None, :, None], col_off[None, :, :, None, :]]
    qg = q.reshape(B, H, rows, ncb, NA_QC, hd)
    ridx = row_idx[:, :, None, None]
    cidx = col_idx[None, None, :, :]
    kg = k.reshape(B, H, rows, GRID_W, hd)[:, :, ridx, cidx]
    vg = v.reshape(B, H, rows, GRID_W, hd)[:, :, ridx, cidx]
    s = jnp.einsum('bhrnqd,bhrinkd->bhrnqik', qg, kg, preferred_element_type=F32) * (hd ** -0.5)
    s = s + bias.astype(F32)
    s = jnp.where(col_ok[:, :, None, :], s, NEG_INF)
    p = jax.nn.softmax(s.reshape(s.shape[:-2] + (kh * kbw,)), axis=-1).reshape(s.shape)
    o = jnp.einsum('bhrnqik,bhrinkd->bhrnqd', p.astype(v.dtype), vg)
    return o.reshape(B, H, S, hd)


def dilated_window_attention(q, k, v, window, dil):
    B, H, S, hd = q.shape
    n = window // (2 * dil)
    L = S // dil
    qb = min(DIL_QB, L)
    nb = -(-L // qb)
    Lp = nb * qb
    kbw = qb + 2 * n

    def to_res(t):
        return t.reshape(B, H, L, dil, hd).transpose(0, 1, 3, 2, 4)

    qr = jnp.pad(to_res(q), ((0, 0), (0, 0), (0, 0), (0, Lp - L), (0, 0)))
    kv_pad = ((0, 0), (0, 0), (0, 0), (n, n + Lp - L), (0, 0))
    kr = jnp.pad(to_res(k), kv_pad)
    vr = jnp.pad(to_res(v), kv_pad)
    idx = np.arange(nb)[:, None] * qb + np.arange(kbw)[None, :]
    kb = kr[:, :, :, idx]
    vb = vr[:, :, :, idx]
    qs = qr.reshape(B, H, dil, nb, qb, hd)
    s = jnp.einsum('bhrnqd,bhrnkd->bhrnqk', qs, kb, preferred_element_type=F32) * (hd ** -0.5)
    m_q = np.arange(nb)[:, None] * qb + np.arange(qb)[None, :]
    m_k = (idx - n)[:, None, :]
    valid = (np.abs(m_k - m_q[:, :, None]) <= n) & (m_k >= 0) & (m_k < L)
    s = jnp.where(valid, s, NEG_INF)
    mx = jnp.max(s, axis=-1, keepdims=True)
    e = jnp.exp(s - mx)
    den = jnp.sum(e, axis=-1, keepdims=True)
    o = jnp.einsum('bhrnqk,bhrnkd->bhrnqd', (e / den).astype(v.dtype), vb)
    lse = (mx + jnp.log(den))[..., 0]
    o = o.reshape(B, H, dil, Lp, hd)[:, :, :, :L].transpose(0, 1, 3, 2, 4).reshape(B, H, S, hd)
    lse = lse.reshape(B, H, dil, Lp)[..., :L].transpose(0, 1, 3, 2).reshape(B, H, S)
    return o, lse


def dilated_mixture(q, k, v):
    outs, lses = [], []
    for g, (window, dil) in enumerate(DIL_GROUPS):
        sl = slice(g * DIL_HEADS_PER_GROUP, (g + 1) * DIL_HEADS_PER_GROUP)
        o, lse = dilated_window_attention(q[:, sl], k[:, sl], v[:, sl], window, dil)
        outs.append(o)
        lses.append(lse)
    w = jax.nn.softmax(jnp.stack(lses), axis=0)
    o = jnp.stack(outs)
    return jnp.sum(w[..., None].astype(o.dtype) * o, axis=0)


def diff_attention(q, k, v, lam, lam_init, norm_g):
    B, H, _, S, d = q.shape
    qb = min(DIFF_QB, S)
    nq = S // qb
    qs = q.reshape(B, H, 2, nq, qb, d).transpose(3, 0, 1, 2, 4, 5)

    def block(qblk):
        s = jnp.einsum('bhcqd,bhckd->bhcqk', qblk, k, preferred_element_type=F32) * (d ** -0.5)
        p = jax.nn.softmax(s, axis=-1)
        a = p[:, :, 0] - lam * p[:, :, 1]
        return jnp.einsum('bhqk,bhke->bhqe', a.astype(v.dtype), v)

    o = lax.map(block, qs)
    o = o.transpose(1, 2, 0, 3, 4).reshape(B, H, S, 2 * d)
    of = o.astype(F32)
    of = of * lax.rsqrt(jnp.mean(jnp.square(of), axis=-1, keepdims=True) + LN_EPS) * norm_g.astype(F32)
    return (of * (1.0 - lam_init)).astype(o.dtype)


def token_mixer(x, lam_init, w_in, b_gates, w_proj_a, w_proj_b, w_proj_c, w_out, rpb, lam_vec, norm_g, rope64, rope_c):
    B, S, D = x.shape
    h = x @ w_in
    sizes = [A_W, A_W, A_W, B_W, B_W, B_W, C_W, C_W, C_W]
    splits = [int(v) for v in np.cumsum(sizes)]
    qa, ka, va, qb, kb, vb, qc, kc, vc, gp = jnp.split(h, splits, axis=-1)

    def heads(t, nh):
        return t.reshape(B, S, nh, -1).transpose(0, 2, 1, 3)

    ya = neighborhood_attention(heads(qa, NA_HEADS), heads(ka, NA_HEADS), heads(va, NA_HEADS), rpb)
    ya = ya.transpose(0, 2, 1, 3).reshape(B, S, A_W)
    cos64, sin64 = rope64
    yb = dilated_mixture(apply_rope(heads(qb, DIL_HEADS), cos64, sin64),
                         apply_rope(heads(kb, DIL_HEADS), cos64, sin64),
                         heads(vb, DIL_HEADS))
    yb = yb.transpose(0, 2, 1, 3).reshape(B, S, B_OUT)
    cosc, sinc = rope_c

    def diff_heads(t):
        return t.reshape(B, S, DIFF_HEADS, 2, DIFF_DIM).transpose(0, 2, 3, 1, 4)

    lv = lam_vec.astype(F32)
    lam = jnp.exp(jnp.sum(lv[0] * lv[1])) - jnp.exp(jnp.sum(lv[2] * lv[3])) + lam_init
    yc = diff_attention(apply_rope(diff_heads(qc), cosc, sinc), apply_rope(diff_heads(kc), cosc, sinc),
                        heads(vc, DIFF_HEADS), lam, lam_init, norm_g)
    yc = yc.transpose(0, 2, 1, 3).reshape(B, S, C_W)
    gates = jax.nn.sigmoid((gp + b_gates).reshape(B, S, N_BRANCH, D))
    merged = gates[:, :, 0] * (ya @ w_proj_a) + gates[:, :, 1] * (yb @ w_proj_b) + gates[:, :, 2] * (yc @ w_proj_c)
    return merged @ w_out


def moe_ffn(x, w_router, b_router, w_gate, b_gate, w_up, b_up, w_down, b_down):
    B, S, D = x.shape
    xt = x.reshape(B * S, D)
    logits = (xt @ w_router).astype(F32) + b_router.astype(F32)
    top_v, top_i = lax.top_k(logits, TOP_K)
    top_w = jax.nn.softmax(top_v, axis=-1)
    gates = jnp.sum(jax.nn.one_hot(top_i, N_EXPERTS, dtype=F32) * top_w[..., None], axis=1).astype(xt.dtype)
    y = jnp.zeros_like(xt)
    for e in range(N_EXPERTS):
        hg = jnp.minimum(xt @ w_gate[e] + b_gate[e], SWIGLU_LIMIT)
        hu = jnp.clip(xt @ w_up[e] + b_up[e], -SWIGLU_LIMIT, SWIGLU_LIMIT)
        hh = hg * jax.nn.sigmoid(SWIGLU_ALPHA * hg) * (hu + 1.0)
        y = y + gates[:, e:e + 1] * (hh @ w_down[e] + b_down[e])
    return y.reshape(B, S, D)


def setup_inputs(seed: int = 0) -> dict:
    key = jax.random.key(seed)
    ks = jax.random.split(key, 24)

    def nrm(k, shape, scale):
        return jax.random.normal(k, shape, F32) * scale

    Lr = DEPTH
    return {
        'x': nrm(ks[0], (BATCH, SEQ, D_MODEL), 1.0),
        'w_in': nrm(ks[1], (Lr, D_MODEL, IN_COLS), D_MODEL ** -0.5),
        'b_gates': nrm(ks[2], (Lr, N_BRANCH * D_MODEL), 0.02),
        'w_proj_a': nrm(ks[3], (Lr, A_W, D_MODEL), A_W ** -0.5),
        'w_proj_b': nrm(ks[4], (Lr, B_OUT, D_MODEL), B_OUT ** -0.5),
        'w_proj_c': nrm(ks[5], (Lr, C_W, D_MODEL), C_W ** -0.5),
        'w_out': nrm(ks[6], (Lr, D_MODEL, D_MODEL), DEEPNORM_BETA * D_MODEL ** -0.5),
        'na_rpb': nrm(ks[7], (Lr, NA_HEADS, 2 * NA_KH - 1, 2 * NA_KW - 1), 0.02),
        'diff_lambda': nrm(ks[8], (Lr, 4, DIFF_DIM), 0.1),
        'diff_norm_g': 1.0 + nrm(ks[9], (Lr, 2 * DIFF_DIM), 0.02),
        'ln1_g': 1.0 + nrm(ks[10], (Lr, D_MODEL), 0.02),
        'ln1_b': nrm(ks[11], (Lr, D_MODEL), 0.02),
        'w_router': nrm(ks[12], (Lr, D_MODEL, N_EXPERTS), D_MODEL ** -0.5),
        'b_router': nrm(ks[13], (Lr, N_EXPERTS), 0.01),
        'w_exp_gate': nrm(ks[14], (Lr, N_EXPERTS, D_MODEL, D_EXPERT), D_MODEL ** -0.5),
        'b_exp_gate': nrm(ks[15], (Lr, N_EXPERTS, D_EXPERT), 0.02),
        'w_exp_up': nrm(ks[16], (Lr, N_EXPERTS, D_MODEL, D_EXPERT), D_MODEL ** -0.5),
        'b_exp_up': nrm(ks[17], (Lr, N_EXPERTS, D_EXPERT), 0.02),
        'w_exp_down': nrm(ks[18], (Lr, N_EXPERTS, D_EXPERT, D_MODEL), DEEPNORM_BETA * D_EXPERT ** -0.5),
        'b_exp_down': nrm(ks[19], (Lr, N_EXPERTS, D_MODEL), 0.02),
        'ln2_g': 1.0 + nrm(ks[20], (Lr, D_MODEL), 0.02),
        'ln2_b': nrm(ks[21], (Lr, D_MODEL), 0.02),
    }


def reference(x, w_in, b_gates, w_proj_a, w_proj_b, w_proj_c, w_out, na_rpb, diff_lambda, diff_norm_g,
              ln1_g, ln1_b, w_router, b_router, w_exp_gate, b_exp_gate, w_exp_up, b_exp_up,
              w_exp_down, b_exp_down, ln2_g, ln2_b):
    S = x.shape[1]
    rope64 = rope_tables(S, HEAD_DIM)
    rope_c = rope_tables(S, DIFF_DIM)
    for l in range(DEPTH):
        lam_init = 0.8 - 0.6 * math.exp(-0.3 * l)
        mix = token_mixer(x, lam_init, w_in[l], b_gates[l], w_proj_a[l], w_proj_b[l], w_proj_c[l], w_out[l],
                          na_rpb[l], diff_lambda[l], diff_norm_g[l], rope64, rope_c)
        x = layer_norm(DEEPNORM_ALPHA * x + mix, ln1_g[l], ln1_b[l])
        ffn = moe_ffn(x, w_router[l], b_router[l], w_exp_gate[l], b_exp_gate[l], w_exp_up[l], b_exp_up[l],
                      w_exp_down[l], b_exp_down[l])
        x = layer_norm(DEEPNORM_ALPHA * x + ffn, ln2_g[l], ln2_b[l])
    return x
```

```python
import functools
import math

import numpy as np
import jax
import jax.numpy as jnp
from jax import lax
from jax.experimental import pallas as pl
from jax.experimental.pallas import tpu as pltpu

F32 = jnp.float32
BF16 = jnp.bfloat16

D_MODEL = 1024
DEPTH = 4
HEAD_DIM = 64
GRID_W = 64
NA_HEADS = 4
NA_KH = 8
NA_KW = 16
DIL_GROUPS = ((128, 1), (512, 4), (2048, 16))
DIL_HALF = 64
DIFF_HEADS = 4
DIFF_DIM = 48
N_EXPERTS = 32
TOP_K = 4
D_EXPERT = 1024
SWIGLU_ALPHA = 1.702
SWIGLU_LIMIT = 7.0
ROPE_THETA = 10000.0
LN_EPS = 1e-5
NEG_INF = -1e30
DEEPNORM_ALPHA = (2 * DEPTH) ** 0.25

LANE = 128
VMEM_LIMIT = 56 * 1024 * 1024

BLK_GATES = 0
BLK_QA, BLK_KA, BLK_VA = 24, 26, 28
BLK_QB, BLK_KB, BLK_VB = 30, 33, 36
BLK_QC, BLK_KC, BLK_VC = 39, 43, 47
N_BLK = 52
NH = N_BLK * LANE
INPROJ_TN = 512
INPROJ_TM = 1024
INPROJ_SUB = 256

_PLAIN, _SCALE8, _ROPE64_Q, _ROPE64_K, _ROPE48_Q, _ROPE48_K, _GATE = range(7)
_BLOCK_KIND = ([_GATE] * 24 + [_SCALE8] * 2 + [_PLAIN] * 4 + [_ROPE64_Q] * 3 + [_ROPE64_K] * 3 + [_PLAIN] * 3
               + [_ROPE48_Q] * 4 + [_ROPE48_K] * 4 + [_PLAIN] * 4 + [_PLAIN])

MOE_TM = 512
ROUTER_T = 512
DISPATCH_T = 256
COMBINE_T = 256


def _cparams(sem):
    return pltpu.CompilerParams(dimension_semantics=sem, vmem_limit_bytes=VMEM_LIMIT)


def _pad_cols(w, width):
    return jnp.pad(w, ((0, 0), (0, width - w.shape[1])))


def _layout_w_in(w_in, b_gates):
    d = w_in.shape[0]
    a_w, b_w, c_w = NA_HEADS * HEAD_DIM, 6 * HEAD_DIM, DIFF_HEADS * 2 * DIFF_DIM
    sizes = [a_w] * 3 + [b_w] * 3 + [c_w] * 3
    offs = np.concatenate([[0], np.cumsum(sizes)])
    qa, ka, va, qb, kb, vb, qc, kc, vc = [w_in[:, offs[i]:offs[i + 1]] for i in range(9)]
    gp = w_in[:, offs[9]:]

    def pad_diff_qk(w):
        w = w.reshape(d, DIFF_HEADS * 2, 2, DIFF_DIM // 2)
        w = jnp.pad(w, ((0, 0), (0, 0), (0, 0), (0, 32 - DIFF_DIM // 2)))
        return w.reshape(d, DIFF_HEADS * 2 * 64)

    def pad_diff_v(w):
        w = w.reshape(d, DIFF_HEADS, 2 * DIFF_DIM)
        w = jnp.pad(w, ((0, 0), (0, 0), (0, LANE - 2 * DIFF_DIM)))
        return w.reshape(d, DIFF_HEADS * LANE)

    cols = [gp, qa, ka, va, qb, kb, vb, pad_diff_qk(qc), pad_diff_qk(kc), pad_diff_v(vc)]
    w = _pad_cols(jnp.concatenate(cols, axis=1), NH)
    bias = _pad_cols(b_gates.reshape(1, -1).astype(F32), NH)
    return w.astype(BF16), bias


def _rope_tables(seq):
    def base(dim):
        inv = 1.0 / (ROPE_THETA ** (jnp.arange(0, dim, 2, dtype=F32) / dim))
        ang = jnp.arange(seq, dtype=F32)[:, None] * inv[None, :]
        return jnp.cos(ang), jnp.sin(ang)

    c64, s64 = base(HEAD_DIM)
    cos64 = jnp.tile(jnp.concatenate([c64, c64], axis=1), (1, 2))
    sin64 = jnp.tile(jnp.concatenate([-s64, s64], axis=1), (1, 2))
    c48, s48 = base(DIFF_DIM)
    pad = 32 - DIFF_DIM // 2
    c48p = jnp.pad(c48, ((0, 0), (0, pad)), constant_values=1.0)
    s48p = jnp.pad(s48, ((0, 0), (0, pad)))
    cos48 = jnp.tile(jnp.concatenate([c48p, c48p], axis=1), (1, 2))
    sin48 = jnp.tile(jnp.concatenate([-s48p, s48p], axis=1), (1, 2))
    return cos64, sin64, cos48, sin48


def _na_bias_table(rpb):
    v = np.arange(NA_KH)[:, None, None, None]
    qc = np.arange(GRID_W)[None, :, None, None]
    i = np.arange(NA_KH)[None, None, :, None]
    kc = np.arange(GRID_W)[None, None, None, :]
    c_start = np.clip(qc - NA_KW // 2, 0, GRID_W - NA_KW)
    ok = (kc >= c_start) & (kc < c_start + NA_KW)
    row_off = np.broadcast_to(i - v + NA_KH - 1, (NA_KH, GRID_W, NA_KH, GRID_W))
    col_off = np.broadcast_to(np.clip(kc - qc + NA_KW - 1, 0, 2 * NA_KW - 2), (NA_KH, GRID_W, NA_KH, GRID_W))
    ok = np.broadcast_to(ok, (NA_KH, GRID_W, NA_KH, GRID_W))
    bias = rpb.astype(F32)[:, row_off, col_off]
    bias = jnp.where(ok[None], bias, NEG_INF)
    return bias.transpose(1, 0, 2, 3, 4).reshape(NA_KH, rpb.shape[0], GRID_W, NA_KH * GRID_W)


def _rope_partner(a):
    lane = lax.broadcasted_iota(jnp.int32, a.shape, 1)
    first_half = (lane % 64) < 32
    return jnp.where(first_half, pltpu.roll(a, 96, 1), pltpu.roll(a, 32, 1))


def _inproj_kernel(x_ref, w_ref, b_ref, c64_ref, s64_ref, c48_ref, s48_ref, o_ref):
    j = pl.program_id(1)
    nblk = INPROJ_TN // LANE
    for jj in range(NH // INPROJ_TN):
        @pl.when(j == jj)
        def _(jj=jj):
            for c in range(INPROJ_TM // INPROJ_SUB):
                r0 = c * INPROJ_SUB
                acc = jnp.dot(x_ref[r0:r0 + INPROJ_SUB, :], w_ref[...], preferred_element_type=F32)
                for blk in range(nblk):
                    kind = _BLOCK_KIND[jj * nblk + blk]
                    a = acc[:, blk * LANE:(blk + 1) * LANE]
                    if kind == _GATE:
                        a = jax.nn.sigmoid(a + b_ref[:, blk * LANE:(blk + 1) * LANE])
                    elif kind == _SCALE8:
                        a = a * (HEAD_DIM ** -0.5)
                    elif kind in (_ROPE64_Q, _ROPE64_K):
                        a = a * c64_ref[r0:r0 + INPROJ_SUB, :] + _rope_partner(a) * s64_ref[r0:r0 + INPROJ_SUB, :]
                        if kind == _ROPE64_Q:
                            a = a * (HEAD_DIM ** -0.5)
                    elif kind in (_ROPE48_Q, _ROPE48_K):
                        a = a * c48_ref[r0:r0 + INPROJ_SUB, :] + _rope_partner(a) * s48_ref[r0:r0 + INPROJ_SUB, :]
                        if kind == _ROPE48_Q:
                            a = a * (DIFF_DIM ** -0.5)
                    o_ref[r0:r0 + INPROJ_SUB, blk * LANE:(blk + 1) * LANE] = a.astype(o_ref.dtype)


def _in_projection(xb, w, bias, tables, seq):
    n = xb.shape[0]
    tm, tn = INPROJ_TM, INPROJ_TN
    per_seq = seq // tm
    tab_spec = pl.BlockSpec((tm, LANE), lambda i, j: (i % per_seq, 0))
    return pl.pallas_call(
        _inproj_kernel,
        grid=(n // tm, NH // tn),
        in_specs=[pl.BlockSpec((tm, D_MODEL), lambda i, j: (i, 0)),
                  pl.BlockSpec((D_MODEL, tn), lambda i, j: (0, j)),
                  pl.BlockSpec((1, tn), lambda i, j: (0, j)),
                  tab_spec, tab_spec, tab_spec, tab_spec],
        out_specs=pl.BlockSpec((tm, tn), lambda i, j: (i, j)),
        out_shape=jax.ShapeDtypeStruct((n, NH), BF16),
        compiler_params=_cparams(("parallel", "arbitrary")),
        name="in_projection",
    )(xb, w, bias, *tables)


NA_ROWS_PER_STEP = 8


def _na_kernel(q_ref, k_ref, v_ref, bias_ref, o_ref):
    rb = pl.program_id(1)
    n_rows = k_ref.shape[1] // GRID_W
    kwin_rows = NA_KH * GRID_W

    def body(rr, carry):
        r = rb * NA_ROWS_PER_STEP + rr
        row_start = jnp.clip(r - NA_KH // 2, 0, n_rows - NA_KH)
        variant = r - row_start
        k0 = pl.multiple_of(row_start * GRID_W, GRID_W)
        q0 = pl.multiple_of(rr * GRID_W, GRID_W)
        q = q_ref[0, pl.ds(q0, GRID_W), :]
        kwin = k_ref[0, pl.ds(k0, kwin_rows), :]
        vwin = v_ref[0, pl.ds(k0, kwin_rows), :]
        outs = []
        for h in range(NA_HEADS):
            sl = slice(h * HEAD_DIM, (h + 1) * HEAD_DIM)
            s = lax.dot_general(q[:, sl], kwin[:, sl], (((1,), (1,)), ((), ())), preferred_element_type=F32)
            s = s + bias_ref[variant, h]
            m = jnp.max(s, axis=-1, keepdims=True)
            e = jnp.exp(s - m)
            p = e / jnp.sum(e, axis=-1, keepdims=True)
            outs.append(jnp.dot(p.astype(BF16), vwin[:, sl], preferred_element_type=F32))
        o_ref[0, pl.ds(q0, GRID_W), :] = jnp.concatenate(outs, axis=-1).astype(o_ref.dtype)
        return carry

    lax.fori_loop(0, NA_ROWS_PER_STEP, body, 0)


def _neighborhood_attention(h3, bias_tab):
    b, s, _ = h3.shape
    rows = s // GRID_W
    qrows = NA_ROWS_PER_STEP * GRID_W
    w = NA_HEADS * HEAD_DIM
    return pl.pallas_call(
        _na_kernel,
        grid=(b, rows // NA_ROWS_PER_STEP),
        in_specs=[pl.BlockSpec((1, qrows, w), lambda bi, ri: (bi, ri, BLK_QA * LANE // w)),
                  pl.BlockSpec((1, s, w), lambda bi, ri: (bi, 0, BLK_KA * LANE // w)),
                  pl.BlockSpec((1, s, w), lambda bi, ri: (bi, 0, BLK_VA * LANE // w)),
                  pl.BlockSpec(bias_tab.shape, lambda bi, ri: (0, 0, 0, 0))],
        out_specs=pl.BlockSpec((1, qrows, w), lambda bi, ri: (bi, ri, 0)),
        out_shape=jax.ShapeDtypeStruct((b, s, w), BF16),
        compiler_params=_cparams(("parallel", "arbitrary")),
        name="neighborhood_attention",
    )(h3, h3, h3, bias_tab)


DIL_QB = 128


def _dilated_kernel(q_ref, k_ref, v_ref, o_ref, lse_ref):
    length = k_ref.shape[1]
    qchunk = q_ref.shape[1]
    kw = min(DIL_QB + 2 * DIL_HALF, length)
    c0 = pl.program_id(2) * qchunk

    def body(qi, carry):
        qoff = pl.multiple_of(qi * DIL_QB, DIL_QB)
        q0 = c0 + qoff
        k0 = pl.multiple_of(jnp.clip(q0 - DIL_HALF, 0, length - kw), DIL_HALF)
        q = q_ref[0, pl.ds(qoff, DIL_QB), :]
        kwin = k_ref[0, pl.ds(k0, kw), :]
        vwin = v_ref[0, pl.ds(k0, kw), :]
        mq = q0 + lax.broadcasted_iota(jnp.int32, (DIL_QB, kw), 0)
        mk = k0 + lax.broadcasted_iota(jnp.int32, (DIL_QB, kw), 1)
        valid = jnp.abs(mk - mq) <= DIL_HALF
        outs, lses = [], []
        for hh in range(2):
            sl = slice(hh * HEAD_DIM, (hh + 1) * HEAD_DIM)
            s = lax.dot_general(q[:, sl], kwin[:, sl], (((1,), (1,)), ((), ())), preferred_element_type=F32)
            s = jnp.where(valid, s, NEG_INF)
            m = jnp.max(s, axis=-1, keepdims=True)
            e = jnp.exp(s - m)
            den = jnp.sum(e, axis=-1, keepdims=True)
            outs.append(jnp.dot((e / den).astype(BF16), vwin[:, sl], preferred_element_type=F32))
            lses.append(jnp.broadcast_to(m + jnp.log(den), (DIL_QB, HEAD_DIM)))
        o_ref[0, pl.ds(qoff, DIL_QB), :] = jnp.concatenate(outs, axis=-1).astype(o_ref.dtype)
        lse_ref[0, pl.ds(qoff, DIL_QB), :] = jnp.concatenate(lses, axis=-1)
        return carry

    lax.fori_loop(0, qchunk // DIL_QB, body, 0)


def _dilated_attention(h3, group, dil):
    b, s, _ = h3.shape
    length = s // dil
    hr = h3.reshape(b, length, dil * NH)
    qchunk = min(length, 1024)
    out_sds = lambda dt: jax.ShapeDtypeStruct((b, length, dil * LANE), dt)
    o, lse = pl.pallas_call(
        _dilated_kernel,
        grid=(b, dil, length // qchunk),
        in_specs=[pl.BlockSpec((1, qchunk, LANE), lambda bi, r, c: (bi, c, r * N_BLK + BLK_QB + group)),
                  pl.BlockSpec((1, length, LANE), lambda bi, r, c: (bi, 0, r * N_BLK + BLK_KB + group)),
                  pl.BlockSpec((1, length, LANE), lambda bi, r, c: (bi, 0, r * N_BLK + BLK_VB + group))],
        out_specs=[pl.BlockSpec((1, qchunk, LANE), lambda bi, r, c: (bi, c, r)),
                   pl.BlockSpec((1, qchunk, LANE), lambda bi, r, c: (bi, c, r))],
        out_shape=[out_sds(BF16), out_sds(F32)],
        compiler_params=_cparams(("parallel", "arbitrary", "arbitrary")),
        name=f"dilated_attention_d{dil}",
    )(hr, hr, hr)
    return o.reshape(b * s, LANE), lse.reshape(b * s, LANE)


DIFF_QB = 256


def _diff_kernel(lam_init_ref, q_ref, k_ref, v_ref, lamv_ref, g_ref, o_ref):
    lv = lamv_ref[...]
    lam = (jnp.exp(jnp.sum(lv[0:1] * lv[1:2], axis=-1, keepdims=True))
           - jnp.exp(jnp.sum(lv[2:3] * lv[3:4], axis=-1, keepdims=True)) + lam_init_ref[0])
    q = q_ref[0]
    k = k_ref[0]
    probs = []
    for c in range(2):
        sl = slice(c * 64, (c + 1) * 64)
        s = lax.dot_general(q[:, sl], k[:, sl], (((1,), (1,)), ((), ())), preferred_element_type=F32)
        m = jnp.max(s, axis=-1, keepdims=True)
        e = jnp.exp(s - m)
        probs.append(e / jnp.sum(e, axis=-1, keepdims=True))
    a = probs[0] - lam * probs[1]
    o = jnp.dot(a.astype(BF16), v_ref[0], preferred_element_type=F32)
    ms = jnp.sum(o * o, axis=-1, keepdims=True) * (1.0 / (2 * DIFF_DIM))
    o_ref[0] = (o * lax.rsqrt(ms + LN_EPS) * g_ref[...]).astype(o_ref.dtype)


def _diff_attention(h3, lam_vec, g_scaled, lam_init):
    b, s, _ = h3.shape
    return pl.pallas_call(
        _diff_kernel,
        grid=(b, DIFF_HEADS, s // DIFF_QB),
        in_specs=[pl.BlockSpec(memory_space=pltpu.SMEM),
                  pl.BlockSpec((1, DIFF_QB, LANE), lambda bi, h, qi: (bi, qi, BLK_QC + h)),
                  pl.BlockSpec((1, s, LANE), lambda bi, h, qi: (bi, 0, BLK_KC + h)),
                  pl.BlockSpec((1, s, LANE), lambda bi, h, qi: (bi, 0, BLK_VC + h)),
                  pl.BlockSpec(lam_vec.shape, lambda bi, h, qi: (0, 0)),
                  pl.BlockSpec((1, LANE), lambda bi, h, qi: (0, 0))],
        out_specs=pl.BlockSpec((1, DIFF_QB, LANE), lambda bi, h, qi: (bi, qi, h)),
        out_shape=jax.ShapeDtypeStruct((b, s, DIFF_HEADS * LANE), BF16),
        compiler_params=_cparams(("parallel", "parallel", "arbitrary")),
        name="diff_attention",
    )(jnp.full((1,), lam_init, F32), h3, h3, h3, lam_vec, g_scaled)


MERGE_TM = 512


def _layer_norm_rows(z, g, b):
    mu = jnp.mean(z, axis=-1, keepdims=True)
    zc = z - mu
    var = jnp.mean(zc * zc, axis=-1, keepdims=True)
    return zc * lax.rsqrt(var + LN_EPS) * g + b


def _merge_kernel(x_ref, g0_ref, g1_ref, g2_ref, ya_ref, o0_ref, o1_ref, o2_ref, l0_ref, l1_ref, l2_ref, yc_ref,
                  wa_ref, wb_ref, wc_ref, wo_ref, lg_ref, lb_ref, x1_ref):
    l0, l1, l2 = l0_ref[...], l1_ref[...], l2_ref[...]
    mx = jnp.maximum(jnp.maximum(l0, l1), l2)
    e0, e1, e2 = jnp.exp(l0 - mx), jnp.exp(l1 - mx), jnp.exp(l2 - mx)
    den = e0 + e1 + e2
    yb = ((e0 / den) * o0_ref[...].astype(F32) + (e1 / den) * o1_ref[...].astype(F32)
          + (e2 / den) * o2_ref[...].astype(F32))
    pa = jnp.dot(ya_ref[...], wa_ref[...], preferred_element_type=F32)
    pb = jnp.dot(yb.astype(BF16), wb_ref[...], preferred_element_type=F32)
    pc = jnp.dot(yc_ref[...], wc_ref[...], preferred_element_type=F32)
    merged = (g0_ref[...].astype(F32) * pa + g1_ref[...].astype(F32) * pb + g2_ref[...].astype(F32) * pc)
    mix = jnp.dot(merged.astype(BF16), wo_ref[...], preferred_element_type=F32)
    z = DEEPNORM_ALPHA * x_ref[...] + mix
    x1_ref[...] = _layer_norm_rows(z, lg_ref[...], lb_ref[...])


def _merge(x, h, ya, dil_outs, yc, wa, wb, wc, wo, ln_g, ln_b):
    n = x.shape[0]
    tm = MERGE_TM
    row = lambda w: pl.BlockSpec((tm, w), lambda i: (i, 0))
    full = lambda a: pl.BlockSpec(a.shape, lambda i: (0,) * a.ndim)
    (o0, l0), (o1, l1), (o2, l2) = dil_outs
    gate_spec = lambda br: pl.BlockSpec((tm, D_MODEL), lambda i, br=br: (i, br))
    return pl.pallas_call(
        _merge_kernel,
        grid=(n // tm,),
        in_specs=[row(D_MODEL), gate_spec(0), gate_spec(1), gate_spec(2), row(ya.shape[1]),
                  row(LANE), row(LANE), row(LANE), row(LANE), row(LANE), row(LANE), row(yc.shape[1]),
                  full(wa), full(wb), full(wc), full(wo), full(ln_g), full(ln_b)],
        out_specs=row(D_MODEL),
        out_shape=jax.ShapeDtypeStruct((n, D_MODEL), F32),
        compiler_params=_cparams(("parallel",)),
        name="merge_proj_ln",
    )(x, h, h, h, ya, o0, o1, o2, l0, l1, l2, yc, wa, wb, wc, wo, ln_g, ln_b)


def _split_bf16(a):
    hi = a.astype(BF16)
    lo = (a - hi.astype(F32)).astype(BF16)
    return hi, lo


def _router_kernel(x_ref, whi_ref, wlo_ref, b_ref, ids_ref, w_ref, rank_ref, cnt_ref, carry_ref):
    @pl.when(pl.program_id(0) == 0)
    def _():
        carry_ref[...] = jnp.zeros_like(carry_ref)

    t = x_ref.shape[0]
    xhi, xlo = _split_bf16(x_ref[...])
    nt = (((1,), (1,)), ((), ()))
    logits = (lax.dot_general(whi_ref[...], xhi, nt, preferred_element_type=F32)
              + lax.dot_general(whi_ref[...], xlo, nt, preferred_element_type=F32)
              + lax.dot_general(wlo_ref[...], xhi, nt, preferred_element_type=F32)) + b_ref[:, 0:1]
    eidx = lax.broadcasted_iota(jnp.int32, (N_EXPERTS, t), 0)
    work = logits
    vals, ids, sels = [], [], []
    for _ in range(TOP_K):
        m = jnp.max(work, axis=0, keepdims=True)
        idx = jnp.min(jnp.where(work == m, eidx, N_EXPERTS), axis=0, keepdims=True)
        sel = eidx == idx
        vals.append(m)
        ids.append(idx)
        sels.append(sel)
        work = jnp.where(sel, -jnp.inf, work)
    exps = [jnp.exp(v - vals[0]) for v in vals]
    den = exps[0] + exps[1] + exps[2] + exps[3]
    mask = (sels[0] | sels[1] | sels[2] | sels[3])
    maskf = mask.astype(F32)
    tri = (lax.broadcasted_iota(jnp.int32, (t, t), 0) <= lax.broadcasted_iota(jnp.int32, (t, t), 1)).astype(BF16)
    incl = jnp.dot(maskf.astype(BF16), tri, preferred_element_type=F32)
    carry = carry_ref[:, 0:1]
    rank_excl = incl - maskf + carry
    for j in range(TOP_K):
        ids_ref[j:j + 1, :] = ids[j]
        w_ref[j:j + 1, :] = exps[j] / den
        rank_ref[j:j + 1, :] = jnp.sum(jnp.where(sels[j], rank_excl, 0.0), axis=0, keepdims=True).astype(jnp.int32)
    new_carry = carry_ref[...] + jnp.sum(maskf, axis=1, keepdims=True)
    carry_ref[...] = new_carry
    cnt_ref[...] = new_carry.astype(jnp.int32)


def _router(x1, wr_hi, wr_lo, b_router):
    n = x1.shape[0]
    t = ROUTER_T
    full = lambda a: pl.BlockSpec(a.shape, lambda i: (0,) * a.ndim)
    tok = pl.BlockSpec((TOP_K, t), lambda i: (0, i))
    return pl.pallas_call(
        _router_kernel,
        grid=(n // t,),
        in_specs=[pl.BlockSpec((t, D_MODEL), lambda i: (i, 0)), full(wr_hi), full(wr_lo), full(b_router)],
        out_specs=[tok, tok, tok, pl.BlockSpec((N_EXPERTS, LANE), lambda i: (0, 0))],
        out_shape=[jax.ShapeDtypeStruct((TOP_K, n), jnp.int32), jax.ShapeDtypeStruct((TOP_K, n), F32),
                   jax.ShapeDtypeStruct((TOP_K, n), jnp.int32), jax.ShapeDtypeStruct((N_EXPERTS, LANE), jnp.int32)],
        scratch_shapes=[pltpu.VMEM((N_EXPERTS, LANE), F32)],
        compiler_params=_cparams(("arbitrary",)),
        name="router_top4",
    )(x1, wr_hi, wr_lo, b_router)


def _slots_kernel(start_ref, ids_ref, rank_ref, slot_ref):
    ids = ids_ref[...]
    acc = rank_ref[...]
    for e in range(N_EXPERTS):
        acc = acc + jnp.where(ids == e, start_ref[e], 0)
    slot_ref[...] = acc


def _slots(starts, ids, rank):
    n = ids.shape[1]
    t = 4096
    tok = pl.BlockSpec((TOP_K, t), lambda i: (0, i))
    return pl.pallas_call(
        _slots_kernel,
        grid=(n // t,),
        in_specs=[pl.BlockSpec(memory_space=pltpu.SMEM), tok, tok],
        out_specs=tok,
        out_shape=jax.ShapeDtypeStruct((TOP_K, n), jnp.int32),
        compiler_params=_cparams(("parallel",)),
        name="expert_slots",
    )(starts, ids, rank)


def _dispatch_kernel(cnt_ref, start_ref, nu_ref, slots_hbm, x_hbm, xs_hbm, slot_smem, slot_sem, row_sem):
    i = pl.program_id(0)
    t = DISPATCH_T
    nrow = TOP_K * t

    def row_copy(src_row, dst_row):
        return pltpu.make_async_copy(x_hbm.at[pl.ds(src_row, 1), :], xs_hbm.at[pl.ds(dst_row, 1), :], row_sem)

    @pl.when(i == 0)
    def _():
        def fill_tile(tile, carry):
            dst = pl.multiple_of(tile * MOE_TM, MOE_TM)
            cp = pltpu.make_async_copy(x_hbm.at[pl.ds(0, MOE_TM), :], xs_hbm.at[pl.ds(dst, MOE_TM), :], row_sem)
            cp.start()
            cp.wait()
            return carry

        lax.fori_loop(nu_ref[0], xs_hbm.shape[0] // MOE_TM, fill_tile, 0)

        def per_expert(e, carry):
            cnt = cnt_ref[e]
            padded = ((cnt + MOE_TM - 1) // MOE_TM) * MOE_TM
            base = start_ref[e]

            def fill(p, c):
                row_copy(0, base + p).start()
                return c

            lax.fori_loop(cnt, padded, fill, 0)

            def drain(p, c):
                row_copy(0, base + p).wait()
                return c

            lax.fori_loop(cnt, padded, drain, 0)
            return carry

        lax.fori_loop(0, N_EXPERTS, per_expert, 0)

    idx_copy = pltpu.make_async_copy(slots_hbm.at[i], slot_smem, slot_sem)
    idx_copy.start()
    idx_copy.wait()

    def issue(tt, carry):
        for j in range(TOP_K):
            row_copy(i * t + tt, slot_smem[j * t + tt]).start()
        return carry

    lax.fori_loop(0, t, issue, 0)

    def drain(q, carry):
        row_copy(0, 0).wait()
        return carry

    lax.fori_loop(0, nrow, drain, 0)


def _dispatch(counts, starts, n_used, slots_tiled, x1, n_slots):
    n = x1.shape[0]
    smem = lambda: pl.BlockSpec(memory_space=pltpu.SMEM)
    return pl.pallas_call(
        _dispatch_kernel,
        grid=(n // DISPATCH_T,),
        in_specs=[smem(), smem(), smem(), pl.BlockSpec(memory_space=pl.ANY), pl.BlockSpec(memory_space=pl.ANY)],
        out_specs=pl.BlockSpec(memory_space=pl.ANY),
        out_shape=jax.ShapeDtypeStruct((n_slots, D_MODEL), x1.dtype),
        scratch_shapes=[pltpu.SMEM((TOP_K * DISPATCH_T,), jnp.int32), pltpu.SemaphoreType.DMA,
                        pltpu.SemaphoreType.DMA],
        compiler_params=_cparams(("arbitrary",)),
        name="moe_dispatch",
    )(counts, starts, n_used, slots_tiled, x1)


EXPERT_CHUNK = 256


def _expert_kernel(te_ref, tb_ref, nu_ref, xs_ref, wg_ref, wu_ref, wd_ref, bg_ref, bu_ref, bd_ref, ys_ref,
                   wgb_ref, wub_ref, wdb_ref):
    i = pl.program_id(0)
    prev = te_ref[jnp.maximum(i - 1, 0)]
    new_expert = (i == 0) | (te_ref[i] != prev)
    valid = i < nu_ref[0]

    @pl.when(new_expert & valid)
    def _():
        wgb_ref[...] = wg_ref[0].astype(BF16)
        wub_ref[...] = wu_ref[0].astype(BF16)
        wdb_ref[...] = wd_ref[0].astype(BF16)

    @pl.when(valid)
    def _():
        x = xs_ref[...].astype(BF16)
        acc = jnp.zeros((MOE_TM, D_MODEL), F32) + bd_ref[0]
        for c in range(D_EXPERT // EXPERT_CHUNK):
            sl = slice(c * EXPERT_CHUNK, (c + 1) * EXPERT_CHUNK)
            hg = jnp.dot(x, wgb_ref[:, sl], preferred_element_type=F32) + bg_ref[0, :, sl]
            hu = jnp.dot(x, wub_ref[:, sl], preferred_element_type=F32) + bu_ref[0, :, sl]
            hg = jnp.minimum(hg, SWIGLU_LIMIT)
            hu = jnp.clip(hu, -SWIGLU_LIMIT, SWIGLU_LIMIT)
            hh = hg * jax.nn.sigmoid(SWIGLU_ALPHA * hg) * (hu + 1.0)
            acc = acc + jnp.dot(hh.astype(BF16), wdb_ref[sl, :], preferred_element_type=F32)
        ys_ref[...] = acc.astype(ys_ref.dtype)

    @pl.when(jnp.logical_not(valid))
    def _():
        ys_ref[...] = jnp.zeros_like(ys_ref)


def _experts(tile_expert, tile_block, n_used, xs, wg, wu, wd, bg, bu, bd):
    n_slots = xs.shape[0]
    n_tiles = n_slots // MOE_TM
    wspec = lambda: pl.BlockSpec((1, D_MODEL, D_EXPERT), lambda i, te, tb, nu: (te[i], 0, 0))
    bspec = lambda: pl.BlockSpec((1, 1, D_EXPERT), lambda i, te, tb, nu: (te[i], 0, 0))
    rows = pl.BlockSpec((MOE_TM, D_MODEL), lambda i, te, tb, nu: (tb[i], 0))
    grid_spec = pltpu.PrefetchScalarGridSpec(
        num_scalar_prefetch=3,
        grid=(n_tiles,),
        in_specs=[rows, wspec(), wspec(), wspec(), bspec(), bspec(), bspec()],
        out_specs=pl.BlockSpec((MOE_TM, D_MODEL), lambda i, te, tb, nu: (i, 0)),
        scratch_shapes=[pltpu.VMEM((D_MODEL, D_EXPERT), BF16), pltpu.VMEM((D_MODEL, D_EXPERT), BF16),
                        pltpu.VMEM((D_EXPERT, D_MODEL), BF16)],
    )
    return pl.pallas_call(
        _expert_kernel,
        grid_spec=grid_spec,
        out_shape=jax.ShapeDtypeStruct((n_slots, D_MODEL), F32),
        compiler_params=_cparams(("arbitrary",)),
        name="moe_experts",
    )(tile_expert, tile_block, n_used, xs, wg, wu, wd, bg, bu, bd)


def _combine_kernel(slots_hbm, ys_hbm, x1_ref, w_ref, lg_ref, lb_ref, x2_ref, slot_smem, buf, slot_sem, row_sem):
    i = pl.program_id(0)
    t = COMBINE_T
    idx_copy = pltpu.make_async_copy(slots_hbm.at[i], slot_smem, slot_sem)
    idx_copy.start()
    idx_copy.wait()

    def row_copy(slot, j, tt):
        return pltpu.make_async_copy(ys_hbm.at[pl.ds(slot, 1), :], buf.at[j, pl.ds(tt, 1), :], row_sem)

    def issue(tt, carry):
        for j in range(TOP_K):
            row_copy(slot_smem[j * t + tt], j, tt).start()
        return carry

    lax.fori_loop(0, t, issue, 0)

    def drain(q, carry):
        row_copy(0, 0, 0).wait()
        return carry

    lax.fori_loop(0, TOP_K * t, drain, 0)

    w = w_ref[...]
    ffn = buf[0] * w[:, 0:1]
    for j in range(1, TOP_K):
        ffn = ffn + buf[j] * w[:, j:j + 1]
    z = DEEPNORM_ALPHA * x1_ref[...] + ffn
    x2_ref[...] = _layer_norm_rows(z, lg_ref[...], lb_ref[...])


def _combine(slots_tiled, ys, x1, w_tok, ln_g, ln_b):
    n = x1.shape[0]
    t = COMBINE_T
    full = lambda a: pl.BlockSpec(a.shape, lambda i: (0,) * a.ndim)
    return pl.pallas_call(
        _combine_kernel,
        grid=(n // t,),
        in_specs=[pl.BlockSpec(memory_space=pl.ANY), pl.BlockSpec(memory_space=pl.ANY),
                  pl.BlockSpec((t, D_MODEL), lambda i: (i, 0)), pl.BlockSpec((t, TOP_K), lambda i: (i, 0)),
                  full(ln_g), full(ln_b)],
        out_specs=pl.BlockSpec((t, D_MODEL), lambda i: (i, 0)),
        out_shape=jax.ShapeDtypeStruct((n, D_MODEL), F32),
        scratch_shapes=[pltpu.SMEM((TOP_K * t,), jnp.int32), pltpu.VMEM((TOP_K, t, D_MODEL), F32),
                        pltpu.SemaphoreType.DMA, pltpu.SemaphoreType.DMA],
        compiler_params=_cparams(("arbitrary",)),
        name="moe_combine_ln",
    )(slots_tiled, ys, x1, w_tok, ln_g, ln_b)


def _tile_slots(slots, t):
    n = slots.shape[1]
    return slots.reshape(TOP_K, n // t, t).transpose(1, 0, 2).reshape(n // t, TOP_K * t)


def _moe(x1, w_router, b_router, wg, bg, wu, bu, wd, bd, ln_g, ln_b):
    n = x1.shape[0]
    n_slots = TOP_K * n + N_EXPERTS * MOE_TM
    n_tiles = n_slots // MOE_TM
    wr_hi, wr_lo = _split_bf16(w_router.T.astype(F32))
    b_r = jnp.broadcast_to(b_router.astype(F32)[:, None], (N_EXPERTS, LANE))
    ids, w_top, rank, cnt = _router(x1, wr_hi, wr_lo, b_r)
    counts = cnt[:, 0]
    padded = ((counts + MOE_TM - 1) // MOE_TM) * MOE_TM
    ends = jnp.cumsum(padded)
    starts = (ends - padded).astype(jnp.int32)
    n_used = (ends[-1] // MOE_TM).astype(jnp.int32)
    tile_first = jnp.arange(n_tiles, dtype=jnp.int32) * MOE_TM
    tile_expert = jnp.minimum(jnp.sum(tile_first[:, None] >= ends[None, :], axis=1), N_EXPERTS - 1).astype(jnp.int32)
    last = n_used - 1
    tile_expert = jnp.where(jnp.arange(n_tiles) < n_used, tile_expert, tile_expert[last])
    tile_block = jnp.minimum(jnp.arange(n_tiles, dtype=jnp.int32), last)
    slots = _slots(starts, ids, rank)
    n_used = n_used.reshape(1)
    xs = _dispatch(counts, starts, n_used, _tile_slots(slots, DISPATCH_T), x1, n_slots)
    ys = _experts(tile_expert, tile_block, n_used, xs, wg, wu, wd,
                  bg[:, None, :], bu[:, None, :], bd[:, None, :])
    return _combine(_tile_slots(slots, COMBINE_T), ys, x1, w_top.T, ln_g.reshape(1, -1), ln_b.reshape(1, -1))


def _mixer(x, batch, seq, lam_init, w_in, b_gates, w_proj_a, w_proj_b, w_proj_c, w_out, rpb, lam_vec, norm_g,
           ln_g, ln_b, tables):
    w, bias = _layout_w_in(w_in, b_gates)
    h = _in_projection(x.astype(BF16), w, bias, tables, seq)
    h3 = h.reshape(batch, seq, NH)
    ya = _neighborhood_attention(h3, _na_bias_table(rpb)).reshape(batch * seq, -1)
    dil_outs = [_dilated_attention(h3, g, dil) for g, (_, dil) in enumerate(DIL_GROUPS)]
    g_scaled = _pad_cols((norm_g.astype(F32) * (1.0 - lam_init)).reshape(1, -1), LANE)
    yc = _diff_attention(h3, lam_vec.astype(F32), g_scaled, lam_init).reshape(batch * seq, -1)
    wc = jnp.pad(w_proj_c.reshape(DIFF_HEADS, 2 * DIFF_DIM, D_MODEL), ((0, 0), (0, LANE - 2 * DIFF_DIM), (0, 0)))
    wc = wc.reshape(DIFF_HEADS * LANE, D_MODEL)
    return _merge(x, h, ya, dil_outs, yc, w_proj_a.astype(BF16), w_proj_b.astype(BF16), wc.astype(BF16),
                  w_out.astype(BF16), ln_g.reshape(1, -1), ln_b.reshape(1, -1))


def kernel(x, w_in, b_gates, w_proj_a, w_proj_b, w_proj_c, w_out, na_rpb, diff_lambda, diff_norm_g, ln1_g, ln1_b,
           w_router, b_router, w_exp_gate, b_exp_gate, w_exp_up, b_exp_up, w_exp_down, b_exp_down, ln2_g, ln2_b):
    batch, seq, d = x.shape
    tables = _rope_tables(seq)
    xf = x.reshape(batch * seq, d).astype(F32)
    for l in range(w_in.shape[0]):
        lam_init = 0.8 - 0.6 * math.exp(-0.3 * l)
        x1 = _mixer(xf, batch, seq, lam_init, w_in[l], b_gates[l], w_proj_a[l], w_proj_b[l], w_proj_c[l], w_out[l],
                    na_rpb[l], diff_lambda[l], diff_norm_g[l], ln1_g[l], ln1_b[l], tables)
        xf = _moe(x1, w_router[l], b_router[l], w_exp_gate[l], b_exp_gate[l], w_exp_up[l], b_exp_up[l],
                  w_exp_down[l], b_exp_down[l], ln2_g[l], ln2_b[l])
    return xf.reshape(batch, seq, d).astype(x.dtype)
```
